```python
import math
import jax, jax.numpy as jnp
from jax import lax
import numpy as np

D_MODEL = 1024
BATCH = 32
SEQ = 256
DEPTH = 2
DEC_BATCH = 4
DEC_SEQ = 4096
PAST_LEN = 512

GRID_W = 64
SC_WIDTH = D_MODEL // 2
SC_KERNEL = 3
N_HEADS = 8
QK_DIM = 64
V_DIM = 2 * QK_DIM
ATTN_WIDTH = N_HEADS * V_DIM
CF_WIDTH = D_MODEL // 2
CF_KERNEL = 31
D_FF = 4 * D_MODEL
N_BRANCH = 3
Q_BLOCK = 128
ROPE_BASE = 10000.0
EPS = 1e-6
IN_SIZES = (SC_WIDTH, SC_WIDTH, SC_WIDTH,
            N_HEADS * 2 * QK_DIM, N_HEADS * 2 * QK_DIM, ATTN_WIDTH,
            2 * CF_WIDTH, D_MODEL, D_MODEL, D_MODEL)
IN_COLS = 3 * SC_WIDTH + 2 * N_HEADS * 2 * QK_DIM + ATTN_WIDTH + 2 * CF_WIDTH + N_BRANCH * D_MODEL

kernel_name = "hybrid_diff_flow_gated_branches_step"


def rmsnorm(x, g):
    xf = x.astype(jnp.float32)
    y = xf * lax.rsqrt(jnp.mean(xf * xf, axis=-1, keepdims=True) + EPS)
    return y.astype(x.dtype) * g


def layernorm(x, g, b):
    xf = x.astype(jnp.float32)
    mu = jnp.mean(xf, axis=-1, keepdims=True)
    var = jnp.mean(jnp.square(xf - mu), axis=-1, keepdims=True)
    y = (xf - mu) * lax.rsqrt(var + EPS)
    return y.astype(x.dtype) * g + b


def dwconv(x, w):
    k, ch = w.shape
    return lax.conv_general_dilated(x, w[:, None, :], window_strides=(1,),
                                    padding=[(k // 2, k // 2)],
                                    dimension_numbers=("NWC", "WIO", "NWC"),
                                    feature_group_count=ch)


def split_in(u):
    offs, acc = [], 0
    for s in IN_SIZES[:-1]:
        acc += s
        offs.append(acc)
    return jnp.split(u, offs, axis=-1)


def axial_rope(x, row, col):
    half = QK_DIM // 2
    quarter = half // 2
    inv = ROPE_BASE ** (-jnp.arange(quarter, dtype=jnp.float32) / quarter)

    def rot(xa, pos):
        ang = pos.astype(jnp.float32)[:, None] * inv[None, :]
        cos = jnp.cos(ang)[None, :, None, None, :].astype(x.dtype)
        sin = jnp.sin(ang)[None, :, None, None, :].astype(x.dtype)
        x1, x2 = xa[..., :quarter], xa[..., quarter:]
        return jnp.concatenate([x1 * cos - x2 * sin, x2 * cos + x1 * sin], axis=-1)

    return jnp.concatenate([rot(x[..., :half], row), rot(x[..., half:], col)], axis=-1)


def diff_attention(q, k, v, lam, lam_init, subln_g):
    b, tq = q.shape[0], q.shape[1]
    nb = tq // Q_BLOCK
    qb = q.reshape(b, nb, Q_BLOCK, N_HEADS, 2, QK_DIM).swapaxes(0, 1)
    scale = QK_DIM ** -0.5

    def block(qq):
        s = jnp.einsum("bqhcd,bkhcd->bhcqk", qq, k).astype(jnp.float32) * scale
        p = jax.nn.softmax(s, axis=-1)
        a = p[:, :, 0] - lam * p[:, :, 1]
        return jnp.einsum("bhqk,bkhe->bqhe", a.astype(v.dtype), v)

    o = lax.map(block, qb).swapaxes(0, 1).reshape(b, tq, N_HEADS, V_DIM)
    o = rmsnorm(o, subln_g) * (1.0 - lam_init)
    return o.reshape(b, tq, ATTN_WIDTH)


def trunk_layer(x, mod, lam_init, lw, rope=None, ctx_k=None, ctx_v=None):
    sh1, sc1, gt1, sh2, sc2, gt2 = jnp.split(mod, 6, axis=-1)
    b, t = x.shape[0], x.shape[1]
    h = rmsnorm(x, lw["g_pre_mix"]) * (1 + sc1) + sh1
    bg, cg, xin, q, k, v, cf_u, ga, gb, gc = split_in(h @ lw["w_in"])
    y_a = (bg * dwconv(cg * xin, lw["sc_conv_w"])) @ lw["sc_w_out"]
    q = q.reshape(b, t, N_HEADS, 2, QK_DIM)
    k = k.reshape(b, t, N_HEADS, 2, QK_DIM)
    v = v.reshape(b, t, N_HEADS, V_DIM)
    if rope is not None:
        q = axial_rope(q, rope[0], rope[1])
        k = axial_rope(k, rope[0], rope[1])
    if ctx_k is not None:
        k_all = jnp.concatenate([ctx_k, k], axis=1)
        v_all = jnp.concatenate([ctx_v, v], axis=1)
    else:
        k_all, v_all = k, v
    f32 = jnp.float32
    lam = (jnp.exp(jnp.sum(lw["lam_q1"].astype(f32) * lw["lam_k1"].astype(f32)))
           - jnp.exp(jnp.sum(lw["lam_q2"].astype(f32) * lw["lam_k2"].astype(f32))) + lam_init)
    y_b = diff_attention(q, k_all, v_all, lam, lam_init, lw["subln_g"]) @ lw["attn_w_out"]
    za, zg = jnp.split(cf_u, 2, axis=-1)
    z = dwconv(za * jax.nn.sigmoid(zg), lw["cf_conv_w"]) + lw["cf_conv_b"]
    z = jax.nn.silu(layernorm(z, lw["cf_ln_g"], lw["cf_ln_b"]))
    y_c = z @ lw["cf_w_out"] + lw["cf_b_out"]
    m = jax.nn.sigmoid(ga) * y_a + jax.nn.sigmoid(gb) * y_b + jax.nn.sigmoid(gc) * y_c
    x = x + gt1 * rmsnorm(m @ lw["w_o"], lw["g_post_mix"])
    h = rmsnorm(x, lw["g_pre_mlp"]) * (1 + sc2) + sh2
    f = jnp.square(jax.nn.relu(h @ lw["w_ff1"])) @ lw["w_ff2"]
    x = x + gt2 * rmsnorm(f, lw["g_post_mlp"])
    return x, k, v


def setup_inputs(seed: int = 0) -> dict:
    key = jax.random.key(seed)
    ks = jax.random.split(key, 40)

    def nrm(i, shape, scale):
        return jax.random.normal(ks[i], shape, jnp.float32) * scale

    def gain(i, shape):
        return 1.0 + 0.05 * jax.random.normal(ks[i], shape, jnp.float32)

    D, L = D_MODEL, DEPTH
    return {
        "x_prompt": nrm(0, (BATCH, SEQ, D), 1.0),
        "x_sample": nrm(1, (DEC_BATCH, DEC_SEQ, D), 1.0),
        "cache_k": nrm(2, (DEC_BATCH, L, PAST_LEN, N_HEADS, 2, QK_DIM), 1.0),
        "cache_v": nrm(3, (DEC_BATCH, L, PAST_LEN, N_HEADS, V_DIM), 1.0),
        "c": nrm(4, (DEC_BATCH, D), 1.0),
        "c_ctx": nrm(5, (D,), 1.0),
        "w_mod": nrm(6, (L, D, 6 * D), 0.5 * D ** -0.5),
        "b_mod": nrm(7, (L, 6 * D), 0.02),
        "g_pre_mix": gain(8, (L, D)),
        "g_post_mix": gain(9, (L, D)),
        "g_pre_mlp": gain(10, (L, D)),
        "g_post_mlp": gain(11, (L, D)),
        "w_in": nrm(12, (L, D, IN_COLS), D ** -0.5),
        "sc_conv_w": nrm(13, (L, SC_KERNEL, SC_WIDTH), SC_KERNEL ** -0.5),
        "sc_w_out": nrm(14, (L, SC_WIDTH, D), SC_WIDTH ** -0.5),
        "lam_q1": nrm(15, (L, QK_DIM), 0.1),
        "lam_k1": nrm(16, (L, QK_DIM), 0.1),
        "lam_q2": nrm(17, (L, QK_DIM), 0.1),
        "lam_k2": nrm(18, (L, QK_DIM), 0.1),
        "subln_g": gain(19, (L, V_DIM)),
        "attn_w_out": nrm(20, (L, ATTN_WIDTH, D), ATTN_WIDTH ** -0.5),
        "cf_conv_w": nrm(21, (L, CF_KERNEL, CF_WIDTH), CF_KERNEL ** -0.5),
        "cf_conv_b": nrm(22, (L, CF_WIDTH), 0.02),
        "cf_ln_g": gain(23, (L, CF_WIDTH)),
        "cf_ln_b": nrm(24, (L, CF_WIDTH), 0.02),
        "cf_w_out": nrm(25, (L, CF_WIDTH, D), CF_WIDTH ** -0.5),
        "cf_b_out": nrm(26, (L, D), 0.02),
        "w_o": nrm(27, (L, D, D), D ** -0.5),
        "w_ff1": nrm(28, (L, D, D_FF), D ** -0.5),
        "w_ff2": nrm(29, (L, D_FF, D), D_FF ** -0.5),
    }


def reference(x_prompt, x_sample, cache_k, cache_v, c, c_ctx, w_mod, b_mod,
              g_pre_mix, g_post_mix, g_pre_mlp, g_post_mlp, w_in, sc_conv_w, sc_w_out,
              lam_q1, lam_k1, lam_q2, lam_k2, subln_g, attn_w_out,
              cf_conv_w, cf_conv_b, cf_ln_g, cf_ln_b, cf_w_out, cf_b_out,
              w_o, w_ff1, w_ff2):
    t_lat = x_sample.shape[1]
    rows = t_lat // GRID_W
    row_idx = jnp.repeat(jnp.arange(rows, dtype=jnp.int32), GRID_W)
    col_idx = jnp.tile(jnp.arange(GRID_W, dtype=jnp.int32), rows)
    y_p, y_s = x_prompt, x_sample
    new_k, new_v = [], []
    for l in range(DEPTH):
        lw = dict(g_pre_mix=g_pre_mix[l], g_post_mix=g_post_mix[l],
                  g_pre_mlp=g_pre_mlp[l], g_post_mlp=g_post_mlp[l],
                  w_in=w_in[l], sc_conv_w=sc_conv_w[l], sc_w_out=sc_w_out[l],
                  lam_q1=lam_q1[l], lam_k1=lam_k1[l], lam_q2=lam_q2[l], lam_k2=lam_k2[l],
                  subln_g=subln_g[l], attn_w_out=attn_w_out[l],
                  cf_conv_w=cf_conv_w[l], cf_conv_b=cf_conv_b[l], cf_ln_g=cf_ln_g[l],
                  cf_ln_b=cf_ln_b[l], cf_w_out=cf_w_out[l], cf_b_out=cf_b_out[l],
                  w_o=w_o[l], w_ff1=w_ff1[l], w_ff2=w_ff2[l])
        lam_init = 0.8 - 0.6 * math.exp(-0.3 * l)
        mod_ctx = (jax.nn.silu(c_ctx) @ w_mod[l] + b_mod[l])[None, None, :]
        mod_lat = (jax.nn.silu(c) @ w_mod[l] + b_mod[l])[:, None, :]
        y_p, k_l, v_l = trunk_layer(y_p, mod_ctx, lam_init, lw)
        new_k.append(k_l)
        new_v.append(v_l)
        y_s, _, _ = trunk_layer(y_s, mod_lat, lam_init, lw, rope=(row_idx, col_idx),
                                ctx_k=cache_k[:, l], ctx_v=cache_v[:, l])
    new_cache_k = jnp.stack(new_k, axis=1)
    new_cache_v = jnp.stack(new_v, axis=1)
    return (y_p, y_s, new_cache_k, new_cache_v)
```

```python
import functools
import math

import jax
import jax.numpy as jnp
from jax import lax
from jax.experimental import pallas as pl
from jax.experimental.pallas import tpu as pltpu

F32 = jnp.float32
BF16 = jnp.bfloat16

N_HEADS = 8
QK_DIM = 64
V_DIM = 2 * QK_DIM
HEAD_COLS = 2 * QK_DIM
SC_KERNEL = 3
CF_KERNEL = 31
GRID_W = 64
ROPE_BASE = 10000.0
EPS = 1e-6
HALO = 16
MOD_ROWS = 16
V7X_VMEM_LIMIT = 56 * 1024 * 1024


def _rms(x):
    return x * lax.rsqrt(jnp.mean(x * x, axis=-1, keepdims=True) + EPS)


def _dot(a, b):
    return jnp.dot(a, b, preferred_element_type=F32)


def _const_spec(shape):
    zeros = (0,) * len(shape)
    return pl.BlockSpec(shape, lambda *_: zeros, pipeline_mode=pl.Buffered(1))


def _params(n_axes):
    return pltpu.CompilerParams(dimension_semantics=("arbitrary",) * n_axes,
                                vmem_limit_bytes=V7X_VMEM_LIMIT)


def _mod_kernel(c_ref, w_ref, b_ref, o_ref):
    c = c_ref[...]
    a = (c * jax.nn.sigmoid(c)).astype(BF16)
    o_ref[...] = _dot(a, w_ref[...].astype(BF16)) + b_ref[...]


def _modulation(c_all, w_mod, b_mod):
    depth, d, n6 = w_mod.shape
    cb = d
    return pl.pallas_call(
        _mod_kernel,
        grid=(depth, n6 // cb),
        in_specs=[pl.BlockSpec((MOD_ROWS, d), lambda l, j: (0, 0)),
                  pl.BlockSpec((None, d, cb), lambda l, j: (l, 0, j)),
                  pl.BlockSpec((None, 1, cb), lambda l, j: (l, 0, j))],
        out_specs=pl.BlockSpec((None, MOD_ROWS, cb), lambda l, j: (l, 0, j)),
        out_shape=jax.ShapeDtypeStruct((depth, MOD_ROWS, n6), F32),
        compiler_params=_params(2),
        name="modulation",
    )(c_all, w_mod, b_mod.reshape(depth, 1, n6))


def _qkv_kernel(*refs, d, rope, emit_cache):
    x_ref, mod_ref, g_ref, w_ref = refs[:4]
    pos = 4
    if rope:
        cos_ref, sa_ref, sb_ref = refs[pos:pos + 3]
        pos += 3
    q_ref, k_ref, v_ref = refs[pos:pos + 3]
    pos += 3
    if emit_cache:
        kc_ref, vc_ref = refs[pos:pos + 2]

    x = x_ref[...]
    mod = mod_ref[...]
    sh1, sc1 = mod[:, 0:d], mod[:, d:2 * d]
    h = (_rms(x) * g_ref[...]) * (1.0 + sc1) + sh1
    u = _dot(h.astype(BF16), w_ref[...])
    q = u[:, 0:d] * (QK_DIM ** -0.5)
    k = u[:, d:2 * d]
    v = u[:, 2 * d:3 * d]
    if rope:
        cos, sa, sb = cos_ref[...], sa_ref[...], sb_ref[...]

        def rot(a):
            outs = []
            for hd in range(a.shape[1] // HEAD_COLS):
                ah = a[:, hd * HEAD_COLS:(hd + 1) * HEAD_COLS]
                up = pltpu.roll(ah, HEAD_COLS - QK_DIM // 4, 1)
                dn = pltpu.roll(ah, QK_DIM // 4, 1)
                outs.append(ah * cos + up * sa + dn * sb)
            return jnp.concatenate(outs, axis=1)

        q = rot(q)
        k = rot(k)
    q_ref[...] = q.astype(BF16)
    k_ref[...] = k.astype(BF16)
    v_ref[...] = v.astype(BF16)
    if emit_cache:
        kc_ref[...] = k
        vc_ref[...] = v


def _qkv(x, mod, mod_row, g_pre, w_qkv, rope_tabs, emit_cache, tb):
    b, t, d = x.shape
    grid = (b, t // tb)
    in_specs = [pl.BlockSpec((None, tb, d), lambda i, j: (i, j, 0)),
                pl.BlockSpec((None, 1, 6 * d), lambda i, j: (mod_row(i), 0, 0)),
                _const_spec((1, d)),
                _const_spec((d, 3 * d))]
    args = [x, mod, g_pre, w_qkv]
    if rope_tabs is not None:
        in_specs += [pl.BlockSpec((tb, HEAD_COLS), lambda i, j: (j, 0))] * 3
        args += list(rope_tabs)
    tok_spec = pl.BlockSpec((None, tb, d), lambda i, j: (i, j, 0))
    out_specs = [tok_spec] * 3
    out_shape = [jax.ShapeDtypeStruct((b, t, d), BF16)] * 3
    if emit_cache:
        out_specs += [tok_spec] * 2
        out_shape += [jax.ShapeDtypeStruct((b, t, d), F32)] * 2
    return pl.pallas_call(
        functools.partial(_qkv_kernel, d=d, rope=rope_tabs is not None, emit_cache=emit_cache),
        grid=grid, in_specs=in_specs, out_specs=out_specs, out_shape=out_shape,
        compiler_params=_params(2), name="qkv",
    )(*args)


def _attn_kernel(*refs, tq, t, kc, has_ctx, lam_init):
    q_ref, k_ref, v_ref = refs[:3]
    pos = 3
    if has_ctx:
        ck_ref, cv_ref = refs[pos:pos + 2]
        pos += 2
    lq1_ref, lk1_ref, lq2_ref, lk2_ref, g_ref, o_ref = refs[pos:pos + 6]

    q = q_ref[...]
    lane = lax.broadcasted_iota(jnp.int32, q.shape, 1)
    zero = jnp.zeros_like(q)
    qq = jnp.concatenate([jnp.where(lane < QK_DIM, q, zero),
                          jnp.where(lane >= QK_DIM, q, zero)], axis=0)

    def step(kblk, vblk, carry):
        m, l, acc = carry
        s = lax.dot_general(qq, kblk, (((1,), (1,)), ((), ())), preferred_element_type=F32)
        m_new = jnp.maximum(m, jnp.max(s, axis=-1, keepdims=True))
        alpha = jnp.exp(m - m_new)
        p = jnp.exp(s - m_new)
        l = alpha * l + jnp.sum(p, axis=-1, keepdims=True)
        acc = alpha * acc + _dot(p.astype(BF16), vblk)
        return m_new, l, acc

    carry = (jnp.full((2 * tq, 1), -jnp.inf, F32),
             jnp.zeros((2 * tq, 1), F32),
             jnp.zeros((2 * tq, V_DIM), F32))
    if has_ctx:
        carry = step(ck_ref[...].astype(BF16), cv_ref[...].astype(BF16), carry)

    def body(i, c):
        start = pl.multiple_of(i * kc, kc)
        return step(k_ref[pl.ds(start, kc), :], v_ref[pl.ds(start, kc), :], c)

    _, l, acc = lax.fori_loop(0, t // kc, body, carry)
    on = acc / l
    lam = (jnp.exp(jnp.sum(lq1_ref[...] * lk1_ref[...], axis=-1, keepdims=True))
           - jnp.exp(jnp.sum(lq2_ref[...] * lk2_ref[...], axis=-1, keepdims=True)) + lam_init)
    o = on[:tq] - lam * on[tq:]
    y = (_rms(o) * g_ref[...]) * (1.0 - lam_init)
    o_ref[...] = y.astype(BF16)


def _attention(q, k, v, ctx, layer, lam_params, subln_g, lam_init, tq, kc):
    b, t, d = q.shape
    grid = (b, N_HEADS, t // tq)
    in_specs = [pl.BlockSpec((None, tq, HEAD_COLS), lambda i, h, j: (i, j, h)),
                pl.BlockSpec((None, t, HEAD_COLS), lambda i, h, j: (i, 0, h)),
                pl.BlockSpec((None, t, V_DIM), lambda i, h, j: (i, 0, h))]
    args = [q, k, v]
    if ctx is not None:
        ck, cv = ctx
        past = ck.shape[2]
        in_specs += [pl.BlockSpec((None, None, past, HEAD_COLS), lambda i, h, j: (i, layer, 0, h)),
                     pl.BlockSpec((None, None, past, V_DIM), lambda i, h, j: (i, layer, 0, h))]
        args += [ck, cv]
    in_specs += [_const_spec((1, QK_DIM))] * 4 + [_const_spec((1, V_DIM))]
    args += list(lam_params) + [subln_g]
    return pl.pallas_call(
        functools.partial(_attn_kernel, tq=tq, t=t, kc=kc, has_ctx=ctx is not None, lam_init=lam_init),
        grid=grid, in_specs=in_specs,
        out_specs=pl.BlockSpec((None, tq, V_DIM), lambda i, h, j: (i, j, h)),
        out_shape=jax.ShapeDtypeStruct((b, t, d), BF16),
        compiler_params=_params(3), name="attn",
    )(*args)


def _dwconv_rows(src_ref, w_ref, taps, first_row, tb, chunk):
    outs = []
    for r0 in range(0, tb, chunk):
        acc = None
        for kk in range(taps):
            term = w_ref[kk:kk + 1, :] * src_ref[pl.ds(first_row + r0 + kk, chunk), :]
            acc = term if acc is None else acc + term
        outs.append(acc)
    return jnp.concatenate(outs, axis=0)


def _mix_kernel(xm_ref, xp_ref, xn_ref, o_ref, mod_ref, gpre_ref, gpost_ref,
                wa_ref, wc_ref, wg_ref, scw_ref, scwo_ref, awo_ref,
                cfw_ref, cfb_ref, lng_ref, lnb_ref, cfwo_ref, cfbo_ref, wo_ref,
                out_ref, pa_scr, pc_scr, *, d, tb, n_t, sc_w, cf_w):
    j = pl.program_id(1)
    xm = xm_ref[...]
    xe = jnp.concatenate([xp_ref[...], xm, xn_ref[...]], axis=0)
    mod = mod_ref[...]
    sh1, sc1, gt1 = mod[:, 0:d], mod[:, d:2 * d], mod[:, 2 * d:3 * d]
    he = ((_rms(xe) * gpre_ref[...]) * (1.0 + sc1) + sh1).astype(BF16)
    row = lax.broadcasted_iota(jnp.int32, (tb + 2 * HALO, 1), 0)
    valid = jnp.logical_and(jnp.logical_or(row >= HALO, j > 0),
                            jnp.logical_or(row < tb + HALO, j < n_t - 1))

    ua = _dot(he, wa_ref[...])
    pa_scr[...] = jnp.where(valid, ua[:, sc_w:2 * sc_w] * ua[:, 2 * sc_w:3 * sc_w], 0.0)
    conv_a = _dwconv_rows(pa_scr, scw_ref, SC_KERNEL, HALO - SC_KERNEL // 2, tb, 32)
    y_a = _dot((ua[HALO:HALO + tb, 0:sc_w] * conv_a).astype(BF16), scwo_ref[...])

    uc = _dot(he, wc_ref[...])
    pc_scr[...] = jnp.where(valid, uc[:, 0:cf_w] * jax.nn.sigmoid(uc[:, cf_w:2 * cf_w]), 0.0)
    z = _dwconv_rows(pc_scr, cfw_ref, CF_KERNEL, HALO - CF_KERNEL // 2, tb, 32) + cfb_ref[...]
    mu = jnp.mean(z, axis=-1, keepdims=True)
    zc = z - mu
    var = jnp.mean(zc * zc, axis=-1, keepdims=True)
    z = (zc * lax.rsqrt(var + EPS)) * lng_ref[...] + lnb_ref[...]
    z = z * jax.nn.sigmoid(z)
    y_c = _dot(z.astype(BF16), cfwo_ref[...]) + cfbo_ref[...]

    y_b = _dot(o_ref[...], awo_ref[...])

    ug = _dot(he[HALO:HALO + tb], wg_ref[...])
    m = (jax.nn.sigmoid(ug[:, 0:d]) * y_a + jax.nn.sigmoid(ug[:, d:2 * d]) * y_b
         + jax.nn.sigmoid(ug[:, 2 * d:3 * d]) * y_c)
    r = _dot(m.astype(BF16), wo_ref[...])
    out_ref[...] = xm + gt1 * (_rms(r) * gpost_ref[...])


def _mix(x, o, mod, mod_row, lw, tb):
    b, t, d = x.shape
    n_t = t // tb
    hb = tb // HALO
    n_hb = t // HALO
    sc_w = lw["sc_w_out"].shape[0]
    cf_w = lw["cf_w_out"].shape[0]
    tok = pl.BlockSpec((None, tb, d), lambda i, j: (i, j, 0))
    in_specs = [tok,
                pl.BlockSpec((None, HALO, d), lambda i, j: (i, jnp.maximum(j * hb - 1, 0), 0)),
                pl.BlockSpec((None, HALO, d), lambda i, j: (i, jnp.minimum((j + 1) * hb, n_hb - 1), 0)),
                tok,
                pl.BlockSpec((None, 1, 6 * d), lambda i, j: (mod_row(i), 0, 0))]
    consts = [lw["g_pre_mix"], lw["g_post_mix"], lw["w_a"], lw["w_c"], lw["w_g"],
              lw["sc_conv_w"], lw["sc_w_out"], lw["attn_w_out"],
              lw["cf_conv_w"], lw["cf_conv_b"], lw["cf_ln_g"], lw["cf_ln_b"],
              lw["cf_w_out"], lw["cf_b_out"], lw["w_o"]]
    in_specs += [_const_spec(a.shape) for a in consts]
    return pl.pallas_call(
        functools.partial(_mix_kernel, d=d, tb=tb, n_t=n_t, sc_w=sc_w, cf_w=cf_w),
        grid=(b, n_t), in_specs=in_specs, out_specs=tok,
        out_shape=jax.ShapeDtypeStruct((b, t, d), F32),
        scratch_shapes=[pltpu.VMEM((tb + 2 * HALO, sc_w), F32),
                        pltpu.VMEM((tb + 2 * HALO, cf_w), F32)],
        compiler_params=_params(2), name="mix",
    )(x, x, x, o, mod, *consts)


def _mlp_kernel(x_ref, mod_ref, gpre_ref, gpost_ref, w1_ref, w2_ref, out_ref, *, d, ff_chunk):
    x = x_ref[...]
    mod = mod_ref[...]
    sh2, sc2, gt2 = mod[:, 3 * d:4 * d], mod[:, 4 * d:5 * d], mod[:, 5 * d:6 * d]
    h = ((_rms(x) * gpre_ref[...]) * (1.0 + sc2) + sh2).astype(BF16)
    f = None
    for c0 in range(0, w1_ref.shape[1], ff_chunk):
        a = jnp.maximum(_dot(h, w1_ref[:, c0:c0 + ff_chunk]), 0.0)
        part = _dot((a * a).astype(BF16), w2_ref[c0:c0 + ff_chunk, :])
        f = part if f is None else f + part
    out_ref[...] = x + gt2 * (_rms(f) * gpost_ref[...])


def _mlp(x, mod, mod_row, lw, tb):
    b, t, d = x.shape
    tok = pl.BlockSpec((None, tb, d), lambda i, j: (i, j, 0))
    consts = [lw["g_pre_mlp"], lw["g_post_mlp"], lw["w_ff1"], lw["w_ff2"]]
    in_specs = [tok, pl.BlockSpec((None, 1, 6 * d), lambda i, j: (mod_row(i), 0, 0))]
    in_specs += [_const_spec(a.shape) for a in consts]
    return pl.pallas_call(
        functools.partial(_mlp_kernel, d=d, ff_chunk=d),
        grid=(b, t // tb), in_specs=in_specs, out_specs=tok,
        out_shape=jax.ShapeDtypeStruct((b, t, d), F32),
        compiler_params=_params(2), name="mlp",
    )(x, mod, *consts)


def _rope_tables(t_lat):
    half, quarter = QK_DIM // 2, QK_DIM // 4
    tpos = jnp.arange(t_lat, dtype=jnp.int32)
    row = (tpos // GRID_W).astype(F32)
    col = (tpos % GRID_W).astype(F32)
    inv = ROPE_BASE ** (-jnp.arange(quarter, dtype=F32) / quarter)
    lane = jnp.arange(HEAD_COLS, dtype=jnp.int32) % QK_DIM
    use_col = (lane >= half)[None, :]
    second = ((lane % half) >= quarter)[None, :]
    ang = jnp.where(use_col, col[:, None], row[:, None]) * inv[lane % quarter][None, :]
    cos, sin = jnp.cos(ang), jnp.sin(ang)
    zero = jnp.zeros_like(sin)
    return cos, jnp.where(second, zero, -sin), jnp.where(second, sin, zero)


def _block_rows(t):
    return min(t, 256)


def _trunk(x_prompt, x_sample, cache_k, cache_v, c, c_ctx, w_mod, b_mod, p):
    depth, d, _ = w_mod.shape
    nb, t_p, _ = x_prompt.shape
    nd, t_s, _ = x_sample.shape
    past = cache_k.shape[2]

    c_all = jnp.concatenate([c_ctx[None, :], c], axis=0)
    c_all = jnp.pad(c_all, ((0, MOD_ROWS - c_all.shape[0]), (0, 0)))
    mod = _modulation(c_all, w_mod, b_mod).reshape(depth, MOD_ROWS, 1, 6 * d)
    ctx = (cache_k.reshape(nd, depth, past, d), cache_v.reshape(nd, depth, past, d))
    rope_tabs = _rope_tables(t_s)

    y_p, y_s = x_prompt, x_sample
    new_k, new_v = [], []
    for l in range(depth):
        lam_init = 0.8 - 0.6 * math.exp(-0.3 * l)
        w_in = p["w_in"][l]
        sc_w = p["sc_w_out"].shape[1]
        cf_w = p["cf_w_out"].shape[1]
        o_q = 3 * sc_w
        o_c = o_q + 3 * d
        o_g = o_c + 2 * cf_w
        row2 = lambda a: a[l][None, :]
        lw = dict(
            g_pre_mix=row2(p["g_pre_mix"]), g_post_mix=row2(p["g_post_mix"]),
            g_pre_mlp=row2(p["g_pre_mlp"]), g_post_mlp=row2(p["g_post_mlp"]),
            w_a=w_in[:, 0:o_q].astype(BF16), w_qkv=w_in[:, o_q:o_c].astype(BF16),
            w_c=w_in[:, o_c:o_g].astype(BF16), w_g=w_in[:, o_g:].astype(BF16),
            sc_conv_w=p["sc_conv_w"][l], sc_w_out=p["sc_w_out"][l].astype(BF16),
            attn_w_out=p["attn_w_out"][l].astype(BF16),
            cf_conv_w=p["cf_conv_w"][l], cf_conv_b=row2(p["cf_conv_b"]),
            cf_ln_g=row2(p["cf_ln_g"]), cf_ln_b=row2(p["cf_ln_b"]),
            cf_w_out=p["cf_w_out"][l].astype(BF16), cf_b_out=row2(p["cf_b_out"]),
            w_o=p["w_o"][l].astype(BF16),
            w_ff1=p["w_ff1"][l].astype(BF16), w_ff2=p["w_ff2"][l].astype(BF16))
        lam_params = [row2(p[n]) for n in ("lam_q1", "lam_k1", "lam_q2", "lam_k2")]
        subln_g = row2(p["subln_g"])
        mod_l = mod[l]

        def run(x, mod_row, rope, ctx_kv, emit_cache):
            t = x.shape[1]
            tb = _block_rows(t)
            res = _qkv(x, mod_l, mod_row, lw["g_pre_mix"], lw["w_qkv"], rope, emit_cache, tb)
            q, k, v = res[:3]
            o = _attention(q, k, v, ctx_kv, l, lam_params, subln_g, lam_init,
                           tq=_block_rows(t), kc=min(t, 512))
            x1 = _mix(x, o, mod_l, mod_row, lw, tb)
            x2 = _mlp(x1, mod_l, mod_row, lw, tb)
            return x2, res[3:]

        y_p, (k_l, v_l) = run(y_p, lambda i: 0, None, None, True)
        new_k.append(k_l)
        new_v.append(v_l)
        y_s, _ = run(y_s, lambda i: i + 1, rope_tabs, ctx, False)

    new_cache_k = jnp.stack(new_k, axis=1).reshape(nb, depth, t_p, N_HEADS, 2, QK_DIM)
    new_cache_v = jnp.stack(new_v, axis=1).reshape(nb, depth, t_p, N_HEADS, V_DIM)
    return y_p, y_s, new_cache_k, new_cache_v


def kernel(x_prompt, x_sample, cache_k, cache_v, c, c_ctx, w_mod, b_mod, g_pre_mix, g_post_mix, g_pre_mlp, g_post_mlp, w_in, sc_conv_w, sc_w_out, lam_q1, lam_k1, lam_q2, lam_k2, subln_g, attn_w_out, cf_conv_w, cf_conv_b, cf_ln_g, cf_ln_b, cf_w_out, cf_b_out, w_o, w_ff1, w_ff2):
    p = dict(g_pre_mix=g_pre_mix, g_post_mix=g_post_mix, g_pre_mlp=g_pre_mlp, g_post_mlp=g_post_mlp,
             w_in=w_in, sc_conv_w=sc_conv_w, sc_w_out=sc_w_out,
             lam_q1=lam_q1, lam_k1=lam_k1, lam_q2=lam_q2, lam_k2=lam_k2,
             subln_g=subln_g, attn_w_out=attn_w_out,
             cf_conv_w=cf_conv_w, cf_conv_b=cf_conv_b, cf_ln_g=cf_ln_g, cf_ln_b=cf_ln_b,
             cf_w_out=cf_w_out, cf_b_out=cf_b_out, w_o=w_o, w_ff1=w_ff1, w_ff2=w_ff2)
    return _trunk(x_prompt, x_sample, cache_k, cache_v, c, c_ctx, w_mod, b_mod, p)
```

```python
import functools
import math

import jax
import jax.numpy as jnp
from jax import lax
from jax.experimental import pallas as pl
from jax.experimental.pallas import tpu as pltpu

F32 = jnp.float32
BF16 = jnp.bfloat16

N_HEADS = 8
QK_DIM = 64
V_DIM = 2 * QK_DIM
HEAD_COLS = 2 * QK_DIM
SC_KERNEL = 3
CF_KERNEL = 31
GRID_W = 64
ROPE_BASE = 10000.0
EPS = 1e-6
LOG2_E = 1.4426950408889634
HALO = 16
MOD_ROWS = 16
V7X_VMEM_LIMIT = 56 * 1024 * 1024


def _rms(x):
    return x * lax.rsqrt(jnp.mean(x * x, axis=-1, keepdims=True) + EPS)


def _dot(a, b):
    return jnp.dot(a, b, preferred_element_type=F32)


def _const_spec(shape):
    zeros = (0,) * len(shape)
    return pl.BlockSpec(shape, lambda *_: zeros, pipeline_mode=pl.Buffered(1))


def _params(n_axes):
    return pltpu.CompilerParams(dimension_semantics=("arbitrary",) * n_axes,
                                vmem_limit_bytes=V7X_VMEM_LIMIT)


def _mod_kernel(c_ref, w_ref, b_ref, o_ref):
    c = c_ref[...]
    a = (c * jax.nn.sigmoid(c)).astype(BF16)
    o_ref[...] = _dot(a, w_ref[...].astype(BF16)) + b_ref[...]


def _modulation(c_all, w_mod, b_mod):
    depth, d, n6 = w_mod.shape
    cb = d
    return pl.pallas_call(
        _mod_kernel,
        grid=(depth, n6 // cb),
        in_specs=[pl.BlockSpec((MOD_ROWS, d), lambda l, j: (0, 0)),
                  pl.BlockSpec((None, d, cb), lambda l, j: (l, 0, j)),
                  pl.BlockSpec((None, 1, cb), lambda l, j: (l, 0, j))],
        out_specs=pl.BlockSpec((None, MOD_ROWS, cb), lambda l, j: (l, 0, j)),
        out_shape=jax.ShapeDtypeStruct((depth, MOD_ROWS, n6), F32),
        compiler_params=_params(2),
        name="modulation",
    )(c_all, w_mod, b_mod.reshape(depth, 1, n6))


def _qkv_kernel(*refs, d, rope, emit_cache):
    x_ref, mod_ref, g_ref, w_ref = refs[:4]
    pos = 4
    if rope:
        cos_ref, sa_ref, sb_ref = refs[pos:pos + 3]
        pos += 3
    q_ref, k_ref, v_ref = refs[pos:pos + 3]
    pos += 3
    if emit_cache:
        kc_ref, vc_ref = refs[pos:pos + 2]

    x = x_ref[...]
    mod = mod_ref[...]
    sh1, sc1 = mod[:, 0:d], mod[:, d:2 * d]
    h = (_rms(x) * g_ref[...]) * (1.0 + sc1) + sh1
    u = _dot(h.astype(BF16), w_ref[...])
    q = u[:, 0:d] * (QK_DIM ** -0.5 * LOG2_E)
    k = u[:, d:2 * d]
    v = u[:, 2 * d:3 * d]
    if rope:
        cos, sa, sb = cos_ref[...], sa_ref[...], sb_ref[...]

        def rot(a):
            outs = []
            for hd in range(a.shape[1] // HEAD_COLS):
                ah = a[:, hd * HEAD_COLS:(hd + 1) * HEAD_COLS]
                up = pltpu.roll(ah, HEAD_COLS - QK_DIM // 4, 1)
                dn = pltpu.roll(ah, QK_DIM // 4, 1)
                outs.append(ah * cos + up * sa + dn * sb)
            return jnp.concatenate(outs, axis=1)

        q = rot(q)
        k = rot(k)
    q_ref[...] = q.astype(BF16)
    k_ref[...] = k.astype(BF16)
    v_ref[...] = v.astype(BF16)
    if emit_cache:
        kc_ref[...] = k
        vc_ref[...] = v


def _qkv(x, mod, mod_row, g_pre, w_qkv, rope_tabs, emit_cache, tb):
    b, t, d = x.shape
    grid = (b, t // tb)
    in_specs = [pl.BlockSpec((None, tb, d), lambda i, j: (i, j, 0)),
                pl.BlockSpec((None, 1, 6 * d), lambda i, j: (mod_row(i), 0, 0)),
                _const_spec((1, d)),
                _const_spec((d, 3 * d))]
    args = [x, mod, g_pre, w_qkv]
    if rope_tabs is not None:
        in_specs += [pl.BlockSpec((tb, HEAD_COLS), lambda i, j: (j, 0))] * 3
        args += list(rope_tabs)
    tok_spec = pl.BlockSpec((None, tb, d), lambda i, j: (i, j, 0))
    out_specs = [tok_spec] * 3
    out_shape = [jax.ShapeDtypeStruct((b, t, d), BF16)] * 3
    if emit_cache:
        out_specs += [tok_spec] * 2
        out_shape += [jax.ShapeDtypeStruct((b, t, d), F32)] * 2
    return pl.pallas_call(
        functools.partial(_qkv_kernel, d=d, rope=rope_tabs is not None, emit_cache=emit_cache),
        grid=grid, in_specs=in_specs, out_specs=out_specs, out_shape=out_shape,
        compiler_params=_params(2), name="qkv",
    )(*args)


def _attn_kernel(*refs, tq, t, kc, has_ctx, lam_init):
    q_ref, k_ref, v_ref = refs[:3]
    pos = 3
    if has_ctx:
        ck_ref, cv_ref = refs[pos:pos + 2]
        pos += 2
    lq1_ref, lk1_ref, lq2_ref, lk2_ref, g_ref, o_ref = refs[pos:pos + 6]

    q = q_ref[...]
    lane = lax.broadcasted_iota(jnp.int32, q.shape, 1)
    zero = jnp.zeros_like(q)
    qq = jnp.concatenate([jnp.where(lane < QK_DIM, q, zero),
                          jnp.where(lane >= QK_DIM, q, zero)], axis=0)

    def step(kblk, vblk, carry):
        m, l, acc = carry
        s = lax.dot_general(qq, kblk, (((1,), (1,)), ((), ())), preferred_element_type=F32)
        m_new = jnp.maximum(m, jnp.max(s, axis=-1, keepdims=True))
        alpha = jnp.exp2(m - m_new)
        p = jnp.exp2(s - m_new)
        l = alpha * l + jnp.sum(p, axis=-1, keepdims=True)
        acc = alpha * acc + _dot(p.astype(BF16), vblk)
        return m_new, l, acc

    carry = (jnp.full((2 * tq, 1), -jnp.inf, F32),
             jnp.zeros((2 * tq, 1), F32),
             jnp.zeros((2 * tq, V_DIM), F32))
    if has_ctx:
        carry = step(ck_ref[...].astype(BF16), cv_ref[...].astype(BF16), carry)

    for c0 in range(0, t, kc):
        carry = step(k_ref[c0:c0 + kc, :], v_ref[c0:c0 + kc, :], carry)
    _, l, acc = carry
    on = acc / l
    lam = (jnp.exp(jnp.sum(lq1_ref[...] * lk1_ref[...], axis=-1, keepdims=True))
           - jnp.exp(jnp.sum(lq2_ref[...] * lk2_ref[...], axis=-1, keepdims=True)) + lam_init)
    o = on[:tq] - lam * on[tq:]
    y = (_rms(o) * g_ref[...]) * (1.0 - lam_init)
    o_ref[...] = y.astype(BF16)


def _attention(q, k, v, ctx, layer, lam_params, subln_g, lam_init, tq, kc):
    b, t, d = q.shape
    grid = (b, N_HEADS, t // tq)
    in_specs = [pl.BlockSpec((None, tq, HEAD_COLS), lambda i, h, j: (i, j, h)),
                pl.BlockSpec((None, t, HEAD_COLS), lambda i, h, j: (i, 0, h)),
                pl.BlockSpec((None, t, V_DIM), lambda i, h, j: (i, 0, h))]
    args = [q, k, v]
    if ctx is not None:
        ck, cv = ctx
        past = ck.shape[2]
        in_specs += [pl.BlockSpec((None, None, past, HEAD_COLS), lambda i, h, j: (i, layer, 0, h)),
                     pl.BlockSpec((None, None, past, V_DIM), lambda i, h, j: (i, layer, 0, h))]
        args += [ck, cv]
    in_specs += [_const_spec((1, QK_DIM))] * 4 + [_const_spec((1, V_DIM))]
    args += list(lam_params) + [subln_g]
    return pl.pallas_call(
        functools.partial(_attn_kernel, tq=tq, t=t, kc=kc, has_ctx=ctx is not None, lam_init=lam_init),
        grid=grid, in_specs=in_specs,
        out_specs=pl.BlockSpec((None, tq, V_DIM), lambda i, h, j: (i, j, h)),
        out_shape=jax.ShapeDtypeStruct((b, t, d), BF16),
        compiler_params=_params(3), name="attn",
    )(*args)


def _dwconv_rows(src_ref, w_ref, taps, first_row, tb, chunk):
    outs = []
    for r0 in range(0, tb, chunk):
        acc = None
        for kk in range(taps):
            term = w_ref[kk:kk + 1, :] * src_ref[pl.ds(first_row + r0 + kk, chunk), :]
            acc = term if acc is None else acc + term
        outs.append(acc)
    return jnp.concatenate(outs, axis=0)


def _mix_kernel(xm_ref, xp_ref, xn_ref, o_ref, mod_ref, gpre_ref, gpost_ref,
                wa_ref, wc_ref, wg_ref, scw_ref, scwo_ref, awo_ref,
                cfw_ref, cfb_ref, lng_ref, lnb_ref, cfwo_ref, cfbo_ref, wo_ref,
                out_ref, pa_scr, pc_scr, *, d, tb, n_t, sc_w, cf_w):
    j = pl.program_id(1)
    xm = xm_ref[...]
    xe = jnp.concatenate([xp_ref[...], xm, xn_ref[...]], axis=0)
    mod = mod_ref[...]
    sh1, sc1, gt1 = mod[:, 0:d], mod[:, d:2 * d], mod[:, 2 * d:3 * d]
    he = ((_rms(xe) * gpre_ref[...]) * (1.0 + sc1) + sh1).astype(BF16)
    row = lax.broadcasted_iota(jnp.int32, (tb + 2 * HALO, 1), 0)
    valid = jnp.logical_and(jnp.logical_or(row >= HALO, j > 0),
                            jnp.logical_or(row < tb + HALO, j < n_t - 1))

    ua = _dot(he, wa_ref[...])
    pa_scr[...] = jnp.where(valid, ua[:, sc_w:2 * sc_w] * ua[:, 2 * sc_w:3 * sc_w], 0.0)
    conv_a = _dwconv_rows(pa_scr, scw_ref, SC_KERNEL, HALO - SC_KERNEL // 2, tb, 32)
    y_a = _dot((ua[HALO:HALO + tb, 0:sc_w] * conv_a).astype(BF16), scwo_ref[...])

    uc = _dot(he, wc_ref[...])
    pc_scr[...] = jnp.where(valid, uc[:, 0:cf_w] * jax.nn.sigmoid(uc[:, cf_w:2 * cf_w]), 0.0)
    z = _dwconv_rows(pc_scr, cfw_ref, CF_KERNEL, HALO - CF_KERNEL // 2, tb, 32) + cfb_ref[...]
    mu = jnp.mean(z, axis=-1, keepdims=True)
    zc = z - mu
    var = jnp.mean(zc * zc, axis=-1, keepdims=True)
    z = (zc * lax.rsqrt(var + EPS)) * lng_ref[...] + lnb_ref[...]
    z = z * jax.nn.sigmoid(z)
    y_c = _dot(z.astype(BF16), cfwo_ref[...]) + cfbo_ref[...]

    y_b = _dot(o_ref[...], awo_ref[...])

    ug = _dot(he[HALO:HALO + tb], wg_ref[...])
    m = (jax.nn.sigmoid(ug[:, 0:d]) * y_a + jax.nn.sigmoid(ug[:, d:2 * d]) * y_b
         + jax.nn.sigmoid(ug[:, 2 * d:3 * d]) * y_c)
    r = _dot(m.astype(BF16), wo_ref[...])
    out_ref[...] = xm + gt1 * (_rms(r) * gpost_ref[...])


def _mix(x, o, mod, mod_row, lw, tb):
    b, t, d = x.shape
    n_t = t // tb
    hb = tb // HALO
    n_hb = t // HALO
    sc_w = lw["sc_w_out"].shape[0]
    cf_w = lw["cf_w_out"].shape[0]
    tok = pl.BlockSpec((None, tb, d), lambda i, j: (i, j, 0))
    in_specs = [tok,
                pl.BlockSpec((None, HALO, d), lambda i, j: (i, jnp.maximum(j * hb - 1, 0), 0)),
                pl.BlockSpec((None, HALO, d), lambda i, j: (i, jnp.minimum((j + 1) * hb, n_hb - 1), 0)),
                tok,
                pl.BlockSpec((None, 1, 6 * d), lambda i, j: (mod_row(i), 0, 0))]
    consts = [lw["g_pre_mix"], lw["g_post_mix"], lw["w_a"], lw["w_c"], lw["w_g"],
              lw["sc_conv_w"], lw["sc_w_out"], lw["attn_w_out"],
              lw["cf_conv_w"], lw["cf_conv_b"], lw["cf_ln_g"], lw["cf_ln_b"],
              lw["cf_w_out"], lw["cf_b_out"], lw["w_o"]]
    in_specs += [_const_spec(a.shape) for a in consts]
    return pl.pallas_call(
        functools.partial(_mix_kernel, d=d, tb=tb, n_t=n_t, sc_w=sc_w, cf_w=cf_w),
        grid=(b, n_t), in_specs=in_specs, out_specs=tok,
        out_shape=jax.ShapeDtypeStruct((b, t, d), F32),
        scratch_shapes=[pltpu.VMEM((tb + 2 * HALO, sc_w), F32),
                        pltpu.VMEM((tb + 2 * HALO, cf_w), F32)],
        compiler_params=_params(2), name="mix",
    )(x, x, x, o, mod, *consts)


def _mlp_kernel(x_ref, mod_ref, gpre_ref, gpost_ref, w1_ref, w2_ref, out_ref, *, d, ff_chunk):
    x = x_ref[...]
    mod = mod_ref[...]
    sh2, sc2, gt2 = mod[:, 3 * d:4 * d], mod[:, 4 * d:5 * d], mod[:, 5 * d:6 * d]
    h = ((_rms(x) * gpre_ref[...]) * (1.0 + sc2) + sh2).astype(BF16)
    f = None
    for c0 in range(0, w1_ref.shape[1], ff_chunk):
        a = jnp.maximum(_dot(h, w1_ref[:, c0:c0 + ff_chunk]), 0.0)
        part = _dot((a * a).astype(BF16), w2_ref[c0:c0 + ff_chunk, :])
        f = part if f is None else f + part
    out_ref[...] = x + gt2 * (_rms(f) * gpost_ref[...])


def _mlp(x, mod, mod_row, lw, tb):
    b, t, d = x.shape
    tok = pl.BlockSpec((None, tb, d), lambda i, j: (i, j, 0))
    consts = [lw["g_pre_mlp"], lw["g_post_mlp"], lw["w_ff1"], lw["w_ff2"]]
    in_specs = [tok, pl.BlockSpec((None, 1, 6 * d), lambda i, j: (mod_row(i), 0, 0))]
    in_specs += [_const_spec(a.shape) for a in consts]
    return pl.pallas_call(
        functools.partial(_mlp_kernel, d=d, ff_chunk=d),
        grid=(b, t // tb), in_specs=in_specs, out_specs=tok,
        out_shape=jax.ShapeDtypeStruct((b, t, d), F32),
        compiler_params=_params(2), name="mlp",
    )(x, mod, *consts)


def _rope_tables(t_lat):
    half, quarter = QK_DIM // 2, QK_DIM // 4
    tpos = jnp.arange(t_lat, dtype=jnp.int32)
    row = (tpos // GRID_W).astype(F32)
    col = (tpos % GRID_W).astype(F32)
    inv = ROPE_BASE ** (-jnp.arange(quarter, dtype=F32) / quarter)
    lane = jnp.arange(HEAD_COLS, dtype=jnp.int32) % QK_DIM
    use_col = (lane >= half)[None, :]
    second = ((lane % half) >= quarter)[None, :]
    ang = jnp.where(use_col, col[:, None], row[:, None]) * inv[lane % quarter][None, :]
    cos, sin = jnp.cos(ang), jnp.sin(ang)
    zero = jnp.zeros_like(sin)
    return cos, jnp.where(second, zero, -sin), jnp.where(second, sin, zero)


def _block_rows(t):
    return min(t, 256)


def _trunk(x_prompt, x_sample, cache_k, cache_v, c, c_ctx, w_mod, b_mod, p):
    depth, d, _ = w_mod.shape
    nb, t_p, _ = x_prompt.shape
    nd, t_s, _ = x_sample.shape
    past = cache_k.shape[2]

    c_all = jnp.concatenate([c_ctx[None, :], c], axis=0)
    c_all = jnp.pad(c_all, ((0, MOD_ROWS - c_all.shape[0]), (0, 0)))
    mod = _modulation(c_all, w_mod, b_mod).reshape(depth, MOD_ROWS, 1, 6 * d)
    ctx = (cache_k.reshape(nd, depth, past, d), cache_v.reshape(nd, depth, past, d))
    rope_tabs = _rope_tables(t_s)

    y_p, y_s = x_prompt, x_sample
    new_k, new_v = [], []
    for l in range(depth):
        lam_init = 0.8 - 0.6 * math.exp(-0.3 * l)
        w_in = p["w_in"][l]
        sc_w = p["sc_w_out"].shape[1]
        cf_w = p["cf_w_out"].shape[1]
        o_q = 3 * sc_w
        o_c = o_q + 3 * d
        o_g = o_c + 2 * cf_w
        row2 = lambda a: a[l][None, :]
        lw = dict(
            g_pre_mix=row2(p["g_pre_mix"]), g_post_mix=row2(p["g_post_mix"]),
            g_pre_mlp=row2(p["g_pre_mlp"]), g_post_mlp=row2(p["g_post_mlp"]),
            w_a=w_in[:, 0:o_q].astype(BF16), w_qkv=w_in[:, o_q:o_c].astype(BF16),
            w_c=w_in[:, o_c:o_g].astype(BF16), w_g=w_in[:, o_g:].astype(BF16),
            sc_conv_w=p["sc_conv_w"][l], sc_w_out=p["sc_w_out"][l].astype(BF16),
            attn_w_out=p["attn_w_out"][l].astype(BF16),
            cf_conv_w=p["cf_conv_w"][l], cf_conv_b=row2(p["cf_conv_b"]),
            cf_ln_g=row2(p["cf_ln_g"]), cf_ln_b=row2(p["cf_ln_b"]),
            cf_w_out=p["cf_w_out"][l].astype(BF16), cf_b_out=row2(p["cf_b_out"]),
            w_o=p["w_o"][l].astype(BF16),
            w_ff1=p["w_ff1"][l].astype(BF16), w_ff2=p["w_ff2"][l].astype(BF16))
        lam_params = [row2(p[n]) for n in ("lam_q1", "lam_k1", "lam_q2", "lam_k2")]
        subln_g = row2(p["subln_g"])
        mod_l = mod[l]

        def run(x, mod_row, rope, ctx_kv, emit_cache):
            t = x.shape[1]
            tb = _block_rows(t)
            res = _qkv(x, mod_l, mod_row, lw["g_pre_mix"], lw["w_qkv"], rope, emit_cache, tb)
            q, k, v = res[:3]
            o = _attention(q, k, v, ctx_kv, l, lam_params, subln_g, lam_init,
                           tq=_block_rows(t), kc=min(t, 512))
            x1 = _mix(x, o, mod_l, mod_row, lw, tb)
            x2 = _mlp(x1, mod_l, mod_row, lw, tb)
            return x2, res[3:]

        y_p, (k_l, v_l) = run(y_p, lambda i: 0, None, None, True)
        new_k.append(k_l)
        new_v.append(v_l)
        y_s, _ = run(y_s, lambda i: i + 1, rope_tabs, ctx, False)

    new_cache_k = jnp.stack(new_k, axis=1).reshape(nb, depth, t_p, N_HEADS, 2, QK_DIM)
    new_cache_v = jnp.stack(new_v, axis=1).reshape(nb, depth, t_p, N_HEADS, V_DIM)
    return y_p, y_s, new_cache_k, new_cache_v


def kernel(x_prompt, x_sample, cache_k, cache_v, c, c_ctx, w_mod, b_mod, g_pre_mix, g_post_mix, g_pre_mlp, g_post_mlp, w_in, sc_conv_w, sc_w_out, lam_q1, lam_k1, lam_q2, lam_k2, subln_g, attn_w_out, cf_conv_w, cf_conv_b, cf_ln_g, cf_ln_b, cf_w_out, cf_b_out, w_o, w_ff1, w_ff2):
    p = dict(g_pre_mix=g_pre_mix, g_post_mix=g_post_mix, g_pre_mlp=g_pre_mlp, g_post_mlp=g_post_mlp,
             w_in=w_in, sc_conv_w=sc_conv_w, sc_w_out=sc_w_out,
             lam_q1=lam_q1, lam_k1=lam_k1, lam_q2=lam_q2, lam_k2=lam_k2,
             subln_g=subln_g, attn_w_out=attn_w_out,
             cf_conv_w=cf_conv_w, cf_conv_b=cf_conv_b, cf_ln_g=cf_ln_g, cf_ln_b=cf_ln_b,
             cf_w_out=cf_w_out, cf_b_out=cf_b_out, w_o=w_o, w_ff1=w_ff1, w_ff2=w_ff2)
    return _trunk(x_prompt, x_sample, cache_k, cache_v, c, c_ctx, w_mod, b_mod, p)
```

```python
import functools
import math

import jax
import jax.numpy as jnp
from jax import lax
from jax.experimental import pallas as pl
from jax.experimental.pallas import tpu as pltpu

F32 = jnp.float32
BF16 = jnp.bfloat16

N_HEADS = 8
QK_DIM = 64
V_DIM = 2 * QK_DIM
HEAD_COLS = 2 * QK_DIM
SC_KERNEL = 3
CF_KERNEL = 31
GRID_W = 64
ROPE_BASE = 10000.0
EPS = 1e-6
LOG2_E = 1.4426950408889634
SUBLANES = 8
KEY_CHUNK = 512
SOFTMAX_SLAB = 64
HALO = 16
MOD_ROWS = 16
V7X_VMEM_LIMIT = 56 * 1024 * 1024


def _rms(x):
    return x * lax.rsqrt(jnp.mean(x * x, axis=-1, keepdims=True) + EPS)


def _dot(a, b):
    return jnp.dot(a, b, preferred_element_type=F32)


def _dot_nt(a, b):
    return lax.dot_general(a, b, (((1,), (1,)), ((), ())), preferred_element_type=F32)


def _const_spec(shape):
    zeros = (0,) * len(shape)
    return pl.BlockSpec(shape, lambda *_: zeros, pipeline_mode=pl.Buffered(1))


def _params(n_axes):
    return pltpu.CompilerParams(dimension_semantics=("arbitrary",) * n_axes,
                                vmem_limit_bytes=V7X_VMEM_LIMIT)


def _mod_kernel(c_ref, w_ref, b_ref, o_ref):
    c = c_ref[...]
    a = (c * jax.nn.sigmoid(c)).astype(BF16)
    o_ref[...] = _dot(a, w_ref[...].astype(BF16)) + b_ref[...]


def _modulation(c_all, w_mod, b_mod):
    depth, d, n6 = w_mod.shape
    cb = d
    return pl.pallas_call(
        _mod_kernel,
        grid=(depth, n6 // cb),
        in_specs=[pl.BlockSpec((MOD_ROWS, d), lambda l, j: (0, 0)),
                  pl.BlockSpec((None, d, cb), lambda l, j: (l, 0, j)),
                  pl.BlockSpec((None, 1, cb), lambda l, j: (l, 0, j))],
        out_specs=pl.BlockSpec((None, MOD_ROWS, cb), lambda l, j: (l, 0, j)),
        out_shape=jax.ShapeDtypeStruct((depth, MOD_ROWS, n6), F32),
        compiler_params=_params(2),
        name="modulation",
    )(c_all, w_mod, b_mod.reshape(depth, 1, n6))


def _qkv_kernel(*refs, d, rope, emit_cache, n_aliased):
    x_ref, mod_ref, g_ref, wqt_ref, wk_ref, wvt_ref = refs[:6]
    pos = 6
    if emit_cache:
        wv_ref = refs[pos]
        pos += 1
    if rope:
        cos_ref, sa_ref, sb_ref, cost_ref, sat_ref, sbt_ref = refs[pos:pos + 6]
        pos += 6
    pos += n_aliased
    qt_ref, k_ref, vt_ref = refs[pos:pos + 3]
    pos += 3
    if emit_cache:
        kc_ref, vc_ref = refs[pos:pos + 2]

    x = x_ref[...]
    mod = mod_ref[...]
    sh1, sc1 = mod[:, 0:d], mod[:, d:2 * d]
    h = ((_rms(x) * g_ref[...]) * (1.0 + sc1) + sh1).astype(BF16)
    qt = _dot_nt(wqt_ref[...], h) * (QK_DIM ** -0.5 * LOG2_E)
    k = _dot(h, wk_ref[...])
    vt = _dot_nt(wvt_ref[...], h)
    if rope:
        cos, sa, sb = cos_ref[...], sa_ref[...], sb_ref[...]
        cost, sat, sbt = cost_ref[...], sat_ref[...], sbt_ref[...]
        shift = QK_DIM // 4
        k_heads, qt_heads = [], []
        for hd in range(d // HEAD_COLS):
            kh = k[:, hd * HEAD_COLS:(hd + 1) * HEAD_COLS]
            up = pltpu.roll(kh, HEAD_COLS - shift, 1)
            dn = pltpu.roll(kh, shift, 1)
            k_heads.append(kh * cos + up * sa + dn * sb)
            qh = qt[hd * HEAD_COLS:(hd + 1) * HEAD_COLS, :]
            up = jnp.concatenate([qh[shift:], qh[:shift]], axis=0)
            dn = jnp.concatenate([qh[-shift:], qh[:-shift]], axis=0)
            qt_heads.append(qh * cost + up * sat + dn * sbt)
        k = jnp.concatenate(k_heads, axis=1)
        qt = jnp.concatenate(qt_heads, axis=0)
    qt_ref[...] = qt.astype(BF16)
    k_ref[...] = k.astype(BF16)
    vt_ref[...] = vt.astype(BF16)
    if emit_cache:
        kc_ref[...] = k
        vc_ref[...] = _dot(h, wv_ref[...])


def _qkv(x, mod, mod_row, lw, rope_tabs, cache, layer, depth, tb):
    b, t, d = x.shape
    emit_cache = cache is not None
    grid = (b, t // tb)
    tok = pl.BlockSpec((None, tb, d), lambda i, j: (i, j, 0))
    tok_t = pl.BlockSpec((None, d, tb), lambda i, j: (i, 0, j))
    in_specs = [tok,
                pl.BlockSpec((None, 1, 6 * d), lambda i, j: (mod_row(i), 0, 0)),
                _const_spec((1, d))] + [_const_spec((d, d))] * 3
    args = [x, mod, lw["g_pre_mix"], lw["w_q_t"], lw["w_k"], lw["w_v_t"]]
    if emit_cache:
        in_specs.append(_const_spec((d, d)))
        args.append(lw["w_v"])
    if rope_tabs is not None:
        in_specs += [pl.BlockSpec((tb, HEAD_COLS), lambda i, j: (j, 0))] * 3
        in_specs += [pl.BlockSpec((HEAD_COLS, tb), lambda i, j: (0, j))] * 3
        args += list(rope_tabs)
    out_specs = [tok_t, tok, pl.BlockSpec((None, None, d, tb), lambda i, j: (i, j, 0, 0))]
    out_shape = [jax.ShapeDtypeStruct((b, d, t), BF16), jax.ShapeDtypeStruct((b, t, d), BF16),
                 jax.ShapeDtypeStruct((b, t // tb, d, tb), BF16)]
    aliases = {}
    if emit_cache:
        out_specs += [pl.BlockSpec((None, None, tb, d), lambda i, j: (i, layer, j, 0))] * 2
        out_shape += [jax.ShapeDtypeStruct((b, depth, t, d), F32)] * 2
        for n_out, arr in enumerate(cache, start=3):
            aliases[len(args)] = n_out
            in_specs.append(pl.BlockSpec(memory_space=pl.ANY))
            args.append(arr)
    return pl.pallas_call(
        functools.partial(_qkv_kernel, d=d, rope=rope_tabs is not None, emit_cache=emit_cache,
                          n_aliased=len(aliases)),
        grid=grid, in_specs=in_specs, out_specs=out_specs, out_shape=out_shape,
        input_output_aliases=aliases, compiler_params=_params(2), name="qkv",
    )(*args)


def _attn_kernel(*refs, tq, n_lat, kc, has_ctx, lam_init):
    qt_ref, k_ref, vt_ref = refs[:3]
    pos = 3
    if has_ctx:
        ck_ref, cv_ref = refs[pos:pos + 2]
        pos += 2
    lq1_ref, lk1_ref, lq2_ref, lk2_ref, g_ref, o_ref = refs[pos:pos + 6]
    s_scr, p_scr = refs[pos + 6:pos + 8], refs[pos + 8:pos + 10]
    off = 1 if has_ctx else 0
    n = n_lat + off
    slab = min(SOFTMAX_SLAB, kc)

    qt = qt_ref[...]
    row = lax.broadcasted_iota(jnp.int32, qt.shape, 0)
    zero = jnp.zeros_like(qt)
    qqt = jnp.concatenate([jnp.where(row < QK_DIM, qt, zero),
                           jnp.where(row >= QK_DIM, qt, zero)], axis=1)

    def scores(c, slot):
        if has_ctx and c == 0:
            kblk = ck_ref[...].astype(BF16)
        else:
            kblk = k_ref[pl.ds((c - off) * kc, kc), :]
        s_scr[slot][...] = _dot(kblk, qqt)

    def softmax(slot, m, l):
        slabs = [pl.ds(g * slab, slab) for g in range(kc // slab)]
        mx = functools.reduce(jnp.maximum, [s_scr[slot][sl, :] for sl in slabs])
        m_new = jnp.maximum(m, jnp.max(mx, axis=0, keepdims=True))
        alpha = jnp.exp2(m - m_new)
        psum = None
        for sl in slabs:
            p = jnp.exp2(s_scr[slot][sl, :] - m_new)
            p_scr[slot][sl, :] = p.astype(BF16)
            psum = p if psum is None else psum + p
        return m_new, alpha * l + jnp.sum(psum, axis=0, keepdims=True), alpha

    def values(c, slot, acc, alpha):
        if has_ctx and c == 0:
            vtblk = jnp.transpose(cv_ref[...]).astype(BF16)
        else:
            vtblk = vt_ref[c - off]
        return alpha * acc + _dot(vtblk, p_scr[slot][...])

    m = jnp.full((1, 2 * tq), -jnp.inf, F32)
    l = jnp.zeros((1, 2 * tq), F32)
    acc = jnp.zeros((V_DIM, 2 * tq), F32)
    alpha_prev = None
    scores(0, 0)
    for i in range(n):
        if i >= 1:
            acc = values(i - 1, (i - 1) % 2, acc, alpha_prev)
        m, l, alpha_prev = softmax(i % 2, m, l)
        if i + 1 < n:
            scores(i + 1, (i + 1) % 2)
    acc = values(n - 1, (n - 1) % 2, acc, alpha_prev)
    on = acc * (1.0 / l)
    lam = (jnp.exp(jnp.sum(lq1_ref[...] * lk1_ref[...], axis=-1, keepdims=True))
           - jnp.exp(jnp.sum(lq2_ref[...] * lk2_ref[...], axis=-1, keepdims=True)) + lam_init)
    ot = on[:, :tq] - lam * on[:, tq:]
    yt = ot * lax.rsqrt(jnp.mean(ot * ot, axis=0, keepdims=True) + EPS)
    yt = (yt * g_ref[...]) * (1.0 - lam_init)
    o_ref[...] = jnp.transpose(yt).astype(BF16)


def _attention(qt, k, vt, ctx, layer, lam_params, subln_g_col, lam_init, tq):
    b, t, d = k.shape
    n_lat, kc = vt.shape[1], vt.shape[3]
    grid = (b, N_HEADS, t // tq)
    in_specs = [pl.BlockSpec((None, HEAD_COLS, tq), lambda i, h, j: (i, h, j)),
                pl.BlockSpec((None, t, HEAD_COLS), lambda i, h, j: (i, 0, h)),
                pl.BlockSpec((None, n_lat, V_DIM, kc), lambda i, h, j: (i, 0, h, 0))]
    args = [qt, k, vt]
    if ctx is not None:
        ck, cv = ctx
        past = ck.shape[2]
        assert past == kc, "the cached context is processed as one key chunk"
        in_specs += [pl.BlockSpec((None, None, past, HEAD_COLS), lambda i, h, j: (i, layer, 0, h)),
                     pl.BlockSpec((None, None, past, V_DIM), lambda i, h, j: (i, layer, 0, h))]
        args += [ck, cv]
    in_specs += [_const_spec((1, QK_DIM))] * 4 + [_const_spec((V_DIM, 1))]
    args += list(lam_params) + [subln_g_col]
    return pl.pallas_call(
        functools.partial(_attn_kernel, tq=tq, n_lat=n_lat, kc=kc, has_ctx=ctx is not None,
                          lam_init=lam_init),
        grid=grid, in_specs=in_specs,
        out_specs=pl.BlockSpec((None, tq, V_DIM), lambda i, h, j: (i, j, h)),
        out_shape=jax.ShapeDtypeStruct((b, t, d), BF16),
        scratch_shapes=[pltpu.VMEM((kc, 2 * tq), F32)] * 2 + [pltpu.VMEM((kc, 2 * tq), BF16)] * 2,
        compiler_params=_params(3), name="attn",
    )(*args)


def _dwconv_rows(src_ref, shifted_ref, w_ref, taps, first_row, tb, chunk):
    n = src_ref.shape[0]
    offs = [first_row + kk for kk in range(taps)]
    for sh in sorted({o % SUBLANES for o in offs} - {0}):
        shifted_ref[sh - 1] = src_ref[pl.ds(sh, n - SUBLANES), :]
    outs = []
    for r0 in range(0, tb, chunk):
        acc = None
        for kk, o in enumerate(offs):
            sh = o % SUBLANES
            base = o - sh + r0
            blk = (src_ref[pl.ds(base, chunk), :] if sh == 0
                   else shifted_ref[sh - 1, pl.ds(base, chunk), :])
            term = w_ref[kk:kk + 1, :] * blk
            acc = term if acc is None else acc + term
        outs.append(acc)
    return jnp.concatenate(outs, axis=0)


def _mix_kernel(xm_ref, xp_ref, xn_ref, o_ref, mod_ref, gpre_ref, gpost_ref,
                wa_ref, wc_ref, wg_ref, scw_ref, scwo_ref, awo_ref,
                cfw_ref, cfb_ref, lng_ref, lnb_ref, cfwo_ref, cfbo_ref, wo_ref,
                out_ref, src_scr, shifted_scr, *, d, tb, n_t, sc_w, cf_w):
    j = pl.program_id(1)
    xm = xm_ref[...]
    xe = jnp.concatenate([xp_ref[...], xm, xn_ref[...]], axis=0)
    mod = mod_ref[...]
    sh1, sc1, gt1 = mod[:, 0:d], mod[:, d:2 * d], mod[:, 2 * d:3 * d]
    he = ((_rms(xe) * gpre_ref[...]) * (1.0 + sc1) + sh1).astype(BF16)
    row = lax.broadcasted_iota(jnp.int32, (tb + 2 * HALO, 1), 0)
    valid = jnp.logical_and(jnp.logical_or(row >= HALO, j > 0),
                            jnp.logical_or(row < tb + HALO, j < n_t - 1))

    ua = _dot(he, wa_ref[...])
    src_scr[...] = jnp.where(valid, ua[:, sc_w:2 * sc_w] * ua[:, 2 * sc_w:3 * sc_w], 0.0)
    conv_a = _dwconv_rows(src_scr, shifted_scr, scw_ref, SC_KERNEL, HALO - SC_KERNEL // 2, tb, 32)
    y_a = _dot((ua[HALO:HALO + tb, 0:sc_w] * conv_a).astype(BF16), scwo_ref[...])

    uc = _dot(he, wc_ref[...])
    src_scr[...] = jnp.where(valid, uc[:, 0:cf_w] * jax.nn.sigmoid(uc[:, cf_w:2 * cf_w]), 0.0)
    z = _dwconv_rows(src_scr, shifted_scr, cfw_ref, CF_KERNEL, HALO - CF_KERNEL // 2, tb, 32)
    z = z + cfb_ref[...]
    mu = jnp.mean(z, axis=-1, keepdims=True)
    zc = z - mu
    var = jnp.mean(zc * zc, axis=-1, keepdims=True)
    z = (zc * lax.rsqrt(var + EPS)) * lng_ref[...] + lnb_ref[...]
    z = z * jax.nn.sigmoid(z)
    y_c = _dot(z.astype(BF16), cfwo_ref[...]) + cfbo_ref[...]

    y_b = _dot(o_ref[...], awo_ref[...])

    ug = _dot(he[HALO:HALO + tb], wg_ref[...])
    m = (jax.nn.sigmoid(ug[:, 0:d]) * y_a + jax.nn.sigmoid(ug[:, d:2 * d]) * y_b
         + jax.nn.sigmoid(ug[:, 2 * d:3 * d]) * y_c)
    r = _dot(m.astype(BF16), wo_ref[...])
    out_ref[...] = xm + gt1 * (_rms(r) * gpost_ref[...])


def _mix(x, o, mod, mod_row, lw, tb):
    b, t, d = x.shape
    n_t = t // tb
    hb = tb // HALO
    n_hb = t // HALO
    sc_w = lw["sc_w_out"].shape[0]
    cf_w = lw["cf_w_out"].shape[0]
    assert sc_w == cf_w, "the two conv branches share their scratch buffers"
    rows = tb + 2 * HALO
    tok = pl.BlockSpec((None, tb, d), lambda i, j: (i, j, 0))
    in_specs = [tok,
                pl.BlockSpec((None, HALO, d), lambda i, j: (i, jnp.maximum(j * hb - 1, 0), 0)),
                pl.BlockSpec((None, HALO, d), lambda i, j: (i, jnp.minimum((j + 1) * hb, n_hb - 1), 0)),
                tok,
                pl.BlockSpec((None, 1, 6 * d), lambda i, j: (mod_row(i), 0, 0))]
    consts = [lw["g_pre_mix"], lw["g_post_mix"], lw["w_a"], lw["w_c"], lw["w_g"],
              lw["sc_conv_w"], lw["sc_w_out"], lw["attn_w_out"],
              lw["cf_conv_w"], lw["cf_conv_b"], lw["cf_ln_g"], lw["cf_ln_b"],
              lw["cf_w_out"], lw["cf_b_out"], lw["w_o"]]
    in_specs += [_const_spec(a.shape) for a in consts]
    return pl.pallas_call(
        functools.partial(_mix_kernel, d=d, tb=tb, n_t=n_t, sc_w=sc_w, cf_w=cf_w),
        grid=(b, n_t), in_specs=in_specs, out_specs=tok,
        out_shape=jax.ShapeDtypeStruct((b, t, d), F32),
        scratch_shapes=[pltpu.VMEM((rows, cf_w), F32),
                        pltpu.VMEM((SUBLANES - 1, rows - SUBLANES, cf_w), F32)],
        compiler_params=_params(2), name="mix",
    )(x, x, x, o, mod, *consts)


def _mlp_kernel(x_ref, mod_ref, gpre_ref, gpost_ref, w1_ref, w2_ref, out_ref, *, d, ff_chunk):
    x = x_ref[...]
    mod = mod_ref[...]
    sh2, sc2, gt2 = mod[:, 3 * d:4 * d], mod[:, 4 * d:5 * d], mod[:, 5 * d:6 * d]
    h = ((_rms(x) * gpre_ref[...]) * (1.0 + sc2) + sh2).astype(BF16)
    f = None
    for c0 in range(0, w1_ref.shape[1], ff_chunk):
        a = jnp.maximum(_dot(h, w1_ref[:, c0:c0 + ff_chunk]), 0.0)
        part = _dot((a * a).astype(BF16), w2_ref[c0:c0 + ff_chunk, :])
        f = part if f is None else f + part
    out_ref[...] = x + gt2 * (_rms(f) * gpost_ref[...])


def _mlp(x, mod, mod_row, lw, tb):
    b, t, d = x.shape
    tok = pl.BlockSpec((None, tb, d), lambda i, j: (i, j, 0))
    consts = [lw["g_pre_mlp"], lw["g_post_mlp"], lw["w_ff1"], lw["w_ff2"]]
    in_specs = [tok, pl.BlockSpec((None, 1, 6 * d), lambda i, j: (mod_row(i), 0, 0))]
    in_specs += [_const_spec(a.shape) for a in consts]
    return pl.pallas_call(
        functools.partial(_mlp_kernel, d=d, ff_chunk=d),
        grid=(b, t // tb), in_specs=in_specs, out_specs=tok,
        out_shape=jax.ShapeDtypeStruct((b, t, d), F32),
        compiler_params=_params(2), name="mlp",
    )(x, mod, *consts)


def _rope_tables(t_lat):
    half, quarter = QK_DIM // 2, QK_DIM // 4
    tpos = jnp.arange(t_lat, dtype=jnp.int32)
    row = (tpos // GRID_W).astype(F32)
    col = (tpos % GRID_W).astype(F32)
    inv = ROPE_BASE ** (-jnp.arange(quarter, dtype=F32) / quarter)
    lane = jnp.arange(HEAD_COLS, dtype=jnp.int32) % QK_DIM
    use_col = (lane >= half)[None, :]
    second = ((lane % half) >= quarter)[None, :]
    ang = jnp.where(use_col, col[:, None], row[:, None]) * inv[lane % quarter][None, :]
    cos, sin = jnp.cos(ang), jnp.sin(ang)
    zero = jnp.zeros_like(sin)
    tabs = (cos, jnp.where(second, zero, -sin), jnp.where(second, sin, zero))
    return tabs + tuple(a.T for a in tabs)


def _block_rows(t):
    return min(t, 256)


def _trunk(x_prompt, x_sample, cache_k, cache_v, c, c_ctx, w_mod, b_mod, p):
    depth, d, _ = w_mod.shape
    nb, t_p, _ = x_prompt.shape
    nd, t_s, _ = x_sample.shape
    past = cache_k.shape[2]

    c_all = jnp.concatenate([c_ctx[None, :], c], axis=0)
    c_all = jnp.pad(c_all, ((0, MOD_ROWS - c_all.shape[0]), (0, 0)))
    mod = _modulation(c_all, w_mod, b_mod).reshape(depth, MOD_ROWS, 1, 6 * d)
    ctx = (cache_k.reshape(nd, depth, past, d), cache_v.reshape(nd, depth, past, d))
    rope_tabs = _rope_tables(t_s)

    y_p, y_s = x_prompt, x_sample
    new_cache = ()
    for l in range(depth):
        lam_init = 0.8 - 0.6 * math.exp(-0.3 * l)
        w_in = p["w_in"][l]
        sc_w = p["sc_w_out"].shape[1]
        cf_w = p["cf_w_out"].shape[1]
        o_q = 3 * sc_w
        o_c = o_q + 3 * d
        o_g = o_c + 2 * cf_w
        row2 = lambda a: a[l][None, :]
        lw = dict(
            g_pre_mix=row2(p["g_pre_mix"]), g_post_mix=row2(p["g_post_mix"]),
            g_pre_mlp=row2(p["g_pre_mlp"]), g_post_mlp=row2(p["g_post_mlp"]),
            w_a=w_in[:, 0:o_q].astype(BF16),
            w_q_t=w_in[:, o_q:o_q + d].T.astype(BF16),
            w_k=w_in[:, o_q + d:o_q + 2 * d].astype(BF16),
            w_v=w_in[:, o_q + 2 * d:o_c].astype(BF16),
            w_v_t=w_in[:, o_q + 2 * d:o_c].T.astype(BF16),
            w_c=w_in[:, o_c:o_g].astype(BF16), w_g=w_in[:, o_g:].astype(BF16),
            sc_conv_w=p["sc_conv_w"][l], sc_w_out=p["sc_w_out"][l].astype(BF16),
            attn_w_out=p["attn_w_out"][l].astype(BF16),
            cf_conv_w=p["cf_conv_w"][l], cf_conv_b=row2(p["cf_conv_b"]),
            cf_ln_g=row2(p["cf_ln_g"]), cf_ln_b=row2(p["cf_ln_b"]),
            cf_w_out=p["cf_w_out"][l].astype(BF16), cf_b_out=row2(p["cf_b_out"]),
            w_o=p["w_o"][l].astype(BF16),
            w_ff1=p["w_ff1"][l].astype(BF16), w_ff2=p["w_ff2"][l].astype(BF16))
        lam_params = [row2(p[n]) for n in ("lam_q1", "lam_k1", "lam_q2", "lam_k2")]
        subln_g_col = p["subln_g"][l][:, None]
        mod_l = mod[l]

        def run(x, mod_row, rope, ctx_kv, cache):
            t = x.shape[1]
            tb = _block_rows(t)
            res = _qkv(x, mod_l, mod_row, lw, rope, cache, l, depth, min(t, KEY_CHUNK))
            qt, k, vt = res[:3]
            o = _attention(qt, k, vt, ctx_kv, l, lam_params, subln_g_col, lam_init,
                           tq=_block_rows(t))
            x1 = _mix(x, o, mod_l, mod_row, lw, tb)
            x2 = _mlp(x1, mod_l, mod_row, lw, tb)
            return x2, res[3:]

        y_p, new_cache = run(y_p, lambda i: 0, None, None, new_cache)
        y_s, _ = run(y_s, lambda i: i + 1, rope_tabs, ctx, None)

    new_cache_k = new_cache[0].reshape(nb, depth, t_p, N_HEADS, 2, QK_DIM)
    new_cache_v = new_cache[1].reshape(nb, depth, t_p, N_HEADS, V_DIM)
    return y_p, y_s, new_cache_k, new_cache_v


def kernel(x_prompt, x_sample, cache_k, cache_v, c, c_ctx, w_mod, b_mod, g_pre_mix, g_post_mix, g_pre_mlp, g_post_mlp, w_in, sc_conv_w, sc_w_out, lam_q1, lam_k1, lam_q2, lam_k2, subln_g, attn_w_out, cf_conv_w, cf_conv_b, cf_ln_g, cf_ln_b, cf_w_out, cf_b_out, w_o, w_ff1, w_ff2):
    p = dict(g_pre_mix=g_pre_mix, g_post_mix=g_post_mix, g_pre_mlp=g_pre_mlp, g_post_mlp=g_post_mlp,
             w_in=w_in, sc_conv_w=sc_conv_w, sc_w_out=sc_w_out,
             lam_q1=lam_q1, lam_k1=lam_k1, lam_q2=lam_q2, lam_k2=lam_k2,
             subln_g=subln_g, attn_w_out=attn_w_out,
             cf_conv_w=cf_conv_w, cf_conv_b=cf_conv_b, cf_ln_g=cf_ln_g, cf_ln_b=cf_ln_b,
             cf_w_out=cf_w_out, cf_b_out=cf_b_out, w_o=w_o, w_ff1=w_ff1, w_ff2=w_ff2)
    return _trunk(x_prompt, x_sample, cache_k, cache_v, c, c_ctx, w_mod, b_mod, p)
```
